```python
import jax, jax.numpy as jnp
from jax import lax
import numpy as np

D_MODEL = 2048
BATCH = 2
SEQ = 8192
DEPTH = 1

MLA_HEADS = 8
MLA_Q_RANK = 512
MLA_KV_RANK = 256
MLA_NOPE_DIM = 128
MLA_ROPE_DIM = 64
MLA_QK_DIM = MLA_NOPE_DIM + MLA_ROPE_DIM
MLA_V_DIM = 128
MLA_WIDTH = MLA_HEADS * MLA_V_DIM
ROPE_THETA = 10000.0
Q_BLOCK = 128
NEG_INF = -1e30

GLA_HEADS = 4
GLA_VALUE_DIM = D_MODEL // 2
GLA_KEY_DIM = GLA_VALUE_DIM // 2
GLA_HEAD_K = GLA_KEY_DIM // GLA_HEADS
GLA_HEAD_V = GLA_VALUE_DIM // GLA_HEADS
GLA_GATE_RANK = 16
GLA_GATE_TAU = 16.0
GLA_CHUNK = 64

MIX_WIDTH = MLA_WIDTH + GLA_VALUE_DIM
IN_SPLITS = (MLA_Q_RANK, MLA_KV_RANK, MLA_ROPE_DIM,
             GLA_KEY_DIM, GLA_KEY_DIM, GLA_VALUE_DIM, GLA_GATE_RANK, GLA_VALUE_DIM)
IN_COLS = sum(IN_SPLITS)

D_FF = -(-8 * D_MODEL // (3 * 256)) * 256
N_MOD = 6
RMS_EPS = 1e-6

kernel_name = "hybrid_mla_gla_sandwich_adaln_block"


def rms_norm(x, w, eps=RMS_EPS):
    xf = x.astype(jnp.float32)
    y = xf * lax.rsqrt(jnp.mean(xf * xf, axis=-1, keepdims=True) + eps)
    return (y * w.astype(jnp.float32)).astype(x.dtype)


def split_cols(z, sizes):
    out, start = [], 0
    for s in sizes:
        out.append(z[..., start:start + s])
        start += s
    return out


def rope_cos_sin(positions, dim):
    inv_freq = 1.0 / (ROPE_THETA ** (jnp.arange(0, dim, 2, dtype=jnp.float32) / dim))
    ang = positions.astype(jnp.float32)[..., None] * inv_freq
    return jnp.cos(ang), jnp.sin(ang)


def apply_rope(x, cos, sin):
    half = x.shape[-1] // 2
    x1 = x[..., :half].astype(jnp.float32)
    x2 = x[..., half:].astype(jnp.float32)
    out = jnp.concatenate([x1 * cos - x2 * sin, x2 * cos + x1 * sin], axis=-1)
    return out.astype(x.dtype)


def mla_mixer(c_q, c_kv, k_rope, positions, g_q, w_uq, g_kv, w_uk, w_uv):
    B, S, _ = c_q.shape
    H = MLA_HEADS
    q = (rms_norm(c_q, g_q) @ w_uq).reshape(B, S, H, MLA_QK_DIM)
    c_kv = rms_norm(c_kv, g_kv)
    k_nope = (c_kv @ w_uk).reshape(B, S, H, MLA_NOPE_DIM)
    v = (c_kv @ w_uv).reshape(B, S, H, MLA_V_DIM)
    cos, sin = rope_cos_sin(positions, MLA_ROPE_DIM)
    q_rope = apply_rope(q[..., MLA_NOPE_DIM:], cos[:, :, None, :], sin[:, :, None, :])
    k_rope = apply_rope(k_rope, cos, sin)
    q = jnp.concatenate([q[..., :MLA_NOPE_DIM], q_rope], axis=-1) * (MLA_QK_DIM ** -0.5)
    k = jnp.concatenate(
        [k_nope, jnp.broadcast_to(k_rope[:, :, None, :], (B, S, H, MLA_ROPE_DIM))], axis=-1)
    q = q.transpose(0, 2, 1, 3)
    k = k.transpose(0, 2, 1, 3)
    v = v.transpose(0, 2, 1, 3)
    nb = S // Q_BLOCK
    q_blocks = q.reshape(B, H, nb, Q_BLOCK, MLA_QK_DIM).transpose(2, 0, 1, 3, 4)
    k_pos = jnp.arange(S)

    def attend(args):
        q_blk, blk = args
        q_pos = blk * Q_BLOCK + jnp.arange(Q_BLOCK)
        s = jnp.einsum('bhqd,bhkd->bhqk', q_blk, k).astype(jnp.float32)
        s = jnp.where(k_pos[None, :] <= q_pos[:, None], s, NEG_INF)
        p = jax.nn.softmax(s, axis=-1).astype(v.dtype)
        return jnp.einsum('bhqk,bhkd->bhqd', p, v)

    o = lax.map(attend, (q_blocks, jnp.arange(nb)))
    return o.transpose(1, 0, 3, 2, 4).reshape(B, S, MLA_WIDTH)


def gla_mixer(q, k, v, a_lr, r, w_gate_up, b_gate, g_gla):
    B, S, _ = q.shape
    H, dk, dv, C = GLA_HEADS, GLA_HEAD_K, GLA_HEAD_V, GLA_CHUNK
    N = S // C
    f32 = jnp.float32
    log_a = jax.nn.log_sigmoid((a_lr @ w_gate_up + b_gate).astype(f32)) / GLA_GATE_TAU

    def chunked(t, d):
        return t.reshape(B, N, C, H, d).transpose(0, 3, 1, 2, 4).astype(f32)

    qc = chunked(q, dk) * (dk ** -0.5)
    kc = chunked(k, dk)
    vc = chunked(v, dv)
    bc = jnp.cumsum(chunked(log_a, dk), axis=3)
    b_last = bc[:, :, :, -1:, :]
    q_dec = qc * jnp.exp(bc)
    k_inv = kc * jnp.exp(-bc)
    k_dec = kc * jnp.exp(b_last - bc)
    causal = jnp.tril(jnp.ones((C, C), dtype=bool))
    attn = jnp.einsum('bhncd,bhnjd->bhncj', q_dec, k_inv)
    attn = jnp.where(causal, attn, 0.0)
    o_intra = jnp.einsum('bhncj,bhnjv->bhncv', attn, vc)
    d_state = jnp.einsum('bhncd,bhncv->bhndv', k_dec, vc)
    decay = jnp.exp(b_last[:, :, :, 0, :])

    def step(state, inp):
        q_n, ds_n, decay_n = inp
        o_n = jnp.einsum('bhcd,bhdv->bhcv', q_n, state)
        state = decay_n[..., None] * state + ds_n
        return state, o_n

    init = jnp.zeros((B, H, dk, dv), f32)
    _, o_inter = lax.scan(step, init, (jnp.moveaxis(q_dec, 2, 0),
                                       jnp.moveaxis(d_state, 2, 0),
                                       jnp.moveaxis(decay, 2, 0)))
    o = o_intra + jnp.moveaxis(o_inter, 0, 2)
    o = o.transpose(0, 2, 3, 1, 4).reshape(B, S, H, dv)
    o = rms_norm(o, g_gla) * jax.nn.silu(r.reshape(B, S, H, dv).astype(f32))
    return o.reshape(B, S, GLA_VALUE_DIM).astype(q.dtype)


def setup_inputs(seed: int = 0) -> dict:
    key = jax.random.key(seed)
    ks = jax.random.split(key, 24)
    f32 = jnp.float32
    L = DEPTH

    def nrm(k, shape, scale):
        return jax.random.normal(k, shape, f32) * scale

    def gain(k, shape):
        return 1.0 + 0.02 * jax.random.normal(k, shape, f32)

    offsets = jax.random.randint(ks[2], (BATCH, 1), 0, 4096, dtype=jnp.int32)
    positions = offsets + jnp.arange(SEQ, dtype=jnp.int32)[None, :]
    return {
        'x': nrm(ks[0], (BATCH, SEQ, D_MODEL), 1.0),
        'c': nrm(ks[1], (BATCH, D_MODEL), 1.0),
        'positions': positions,
        'w_ada': nrm(ks[3], (L, D_MODEL, N_MOD * D_MODEL), D_MODEL ** -0.5),
        'b_ada': nrm(ks[4], (L, N_MOD * D_MODEL), 0.01),
        'g_pre_mix': gain(ks[5], (L, D_MODEL)),
        'g_post_mix': gain(ks[6], (L, D_MODEL)),
        'w_in': nrm(ks[7], (L, D_MODEL, IN_COLS), D_MODEL ** -0.5),
        'g_q': gain(ks[8], (L, MLA_Q_RANK)),
        'w_uq': nrm(ks[9], (L, MLA_Q_RANK, MLA_HEADS * MLA_QK_DIM), MLA_Q_RANK ** -0.5),
        'g_kv': gain(ks[10], (L, MLA_KV_RANK)),
        'w_uk': nrm(ks[11], (L, MLA_KV_RANK, MLA_HEADS * MLA_NOPE_DIM), MLA_KV_RANK ** -0.5),
        'w_uv': nrm(ks[12], (L, MLA_KV_RANK, MLA_HEADS * MLA_V_DIM), MLA_KV_RANK ** -0.5),
        'w_gate_up': nrm(ks[13], (L, GLA_GATE_RANK, GLA_KEY_DIM), GLA_GATE_RANK ** -0.5),
        'b_gate': nrm(ks[14], (L, GLA_KEY_DIM), 0.1),
        'g_gla': gain(ks[15], (L, GLA_HEAD_V)),
        'w_out': nrm(ks[16], (L, MIX_WIDTH, D_MODEL), MIX_WIDTH ** -0.5),
        'g_pre_ffn': gain(ks[17], (L, D_MODEL)),
        'g_post_ffn': gain(ks[18], (L, D_MODEL)),
        'w_ffn_gate': nrm(ks[19], (L, D_MODEL, D_FF), D_MODEL ** -0.5),
        'w_ffn_up': nrm(ks[20], (L, D_MODEL, D_FF), D_MODEL ** -0.5),
        'w_ffn_down': nrm(ks[21], (L, D_FF, D_MODEL), D_FF ** -0.5),
    }


def reference(x, c, positions, w_ada, b_ada, g_pre_mix, g_post_mix, w_in, g_q, w_uq,
              g_kv, w_uk, w_uv, w_gate_up, b_gate, g_gla, w_out, g_pre_ffn, g_post_ffn,
              w_ffn_gate, w_ffn_up, w_ffn_down):
    for l in range(DEPTH):
        ada = jax.nn.silu(c) @ w_ada[l] + b_ada[l]
        shift_m, scale_m, gate_m, shift_f, scale_f, gate_f = [
            t[:, None, :] for t in jnp.split(ada, N_MOD, axis=-1)]

        h = rms_norm(x, g_pre_mix[l]) * (1.0 + scale_m) + shift_m
        z = h @ w_in[l]
        c_q, c_kv, k_rope, q_g, k_g, v_g, a_g, r_g = split_cols(z, IN_SPLITS)
        o_mla = mla_mixer(c_q, c_kv, k_rope, positions, g_q[l], w_uq[l],
                          g_kv[l], w_uk[l], w_uv[l])
        o_gla = gla_mixer(q_g, k_g, v_g, a_g, r_g, w_gate_up[l], b_gate[l], g_gla[l])
        o = jnp.concatenate([o_mla, o_gla], axis=-1) @ w_out[l]
        x = x + gate_m * rms_norm(o, g_post_mix[l])

        h = rms_norm(x, g_pre_ffn[l]) * (1.0 + scale_f) + shift_f
        f = (jax.nn.silu(h @ w_ffn_gate[l]) * (h @ w_ffn_up[l])) @ w_ffn_down[l]
        x = x + gate_f * rms_norm(f, g_post_ffn[l])
    return x
```

```python
import functools

import numpy as np
import jax
import jax.numpy as jnp
from jax import lax
from jax.experimental import pallas as pl
from jax.experimental.pallas import tpu as pltpu

F32 = jnp.float32
BF16 = jnp.bfloat16

MLA_HEADS = 8
MLA_Q_RANK = 512
MLA_KV_RANK = 256
MLA_NOPE = 128
MLA_ROPE = 64
MLA_QK = MLA_NOPE + MLA_ROPE
MLA_V = 128
ROPE_THETA = 10000.0
NEG_INF = -1e30

GLA_HEADS = 4
GLA_DK = 128
GLA_DV = 256
GLA_KEY = GLA_HEADS * GLA_DK
GLA_VAL = GLA_HEADS * GLA_DV
GLA_GATE_RANK = 16
GLA_GATE_TAU = 16.0
GLA_CHUNK = 64

N_MOD = 6
RMS_EPS = 1e-6

LANE = 128
Z_CQ = 0
Z_CKV = 512
Z_KR = 768
Z_AG = 896
Z_QG = 1024
Z_KG = 1536
Z_VG = 2048
Z_RG = 3072
Z_COLS = 4096

VMEM_LIMIT = 56 * 1024 * 1024


def _cparams(sem):
    return pltpu.CompilerParams(dimension_semantics=sem, vmem_limit_bytes=VMEM_LIMIT)


def _rms(x, g):
    return x * lax.rsqrt(jnp.mean(x * x, axis=-1, keepdims=True) + RMS_EPS) * g


def _ada_kernel(c_ref, w_ref, b_ref, o_ref):
    c = c_ref[...]
    s = c * jax.nn.sigmoid(c)
    o_ref[...] = jnp.dot(s, w_ref[...], preferred_element_type=F32) + b_ref[...]


def _ada(c_pad, w_ada, b_ada):
    m, d = c_pad.shape
    n = w_ada.shape[1]
    tn = min(n, 1024)
    return pl.pallas_call(
        _ada_kernel,
        out_shape=jax.ShapeDtypeStruct((m, n), F32),
        grid=(n // tn,),
        in_specs=[pl.BlockSpec((m, d), lambda j: (0, 0)),
                  pl.BlockSpec((d, tn), lambda j: (0, j)),
                  pl.BlockSpec((1, tn), lambda j: (0, j))],
        out_specs=pl.BlockSpec((m, tn), lambda j: (0, j)),
        compiler_params=_cparams(("arbitrary",)),
        name="ada",
    )(c_pad, w_ada, b_ada)


ROW_CHUNK = 128


def _modulated_norm_to(h_ref, x_ref, g_ref, scale_ref, shift_ref):
    g = g_ref[...]
    one_plus = 1.0 + scale_ref[0]
    shift = shift_ref[0]

    def body(r, carry):
        rows = pl.ds(pl.multiple_of(r * ROW_CHUNK, ROW_CHUNK), ROW_CHUNK)
        h_ref[rows, :] = (_rms(x_ref[rows, :], g) * one_plus + shift).astype(BF16)
        return carry

    lax.fori_loop(0, x_ref.shape[0] // ROW_CHUNK, body, 0)


def _premix_kernel(x_ref, g_ref, scale_ref, shift_ref, w_ref, z_ref, h_ref):
    @pl.when(pl.program_id(1) == 0)
    def _():
        _modulated_norm_to(h_ref, x_ref, g_ref, scale_ref, shift_ref)

    z_ref[...] = jnp.dot(h_ref[...], w_ref[...], preferred_element_type=F32).astype(BF16)


def _premix(x2, g, mod, w_in_r, seq, tm, tn):
    t, d = x2.shape
    n = w_in_r.shape[1]
    per_b = seq // tm
    return pl.pallas_call(
        _premix_kernel,
        out_shape=jax.ShapeDtypeStruct((t, n), BF16),
        grid=(t // tm, n // tn),
        in_specs=[pl.BlockSpec((tm, d), lambda i, j: (i, 0)),
                  pl.BlockSpec((1, d), lambda i, j: (0, 0)),
                  pl.BlockSpec((1, 1, d), lambda i, j: ((i // per_b) * N_MOD + 1, 0, 0)),
                  pl.BlockSpec((1, 1, d), lambda i, j: ((i // per_b) * N_MOD + 0, 0, 0)),
                  pl.BlockSpec((d, tn), lambda i, j: (0, j))],
        out_specs=pl.BlockSpec((tm, tn), lambda i, j: (i, j)),
        scratch_shapes=[pltpu.VMEM((tm, d), BF16)],
        compiler_params=_cparams(("arbitrary", "arbitrary")),
        name="premix",
    )(x2, g, mod, mod, w_in_r)


def _mla_prep_kernel(pos_ref, invf_ref, cq_ref, ckv_ref, kr_ref, gq_ref, gkv_ref,
                     wuqT_ref, wuk_ref, wuvT_ref, qT_ref, k_ref, vT_ref):
    half = MLA_ROPE // 2
    ang = invf_ref[...] * pos_ref[0].astype(F32)
    cosT = jnp.cos(ang)
    sinT = jnp.sin(ang)

    nt = (((1,), (1,)), ((), ()))
    cqn = _rms(cq_ref[...].astype(F32), gq_ref[...]).astype(BF16)
    qT = lax.dot_general(wuqT_ref[...], cqn, nt, preferred_element_type=F32)
    scale = MLA_QK ** -0.5
    for h in range(MLA_HEADS):
        base = h * MLA_QK
        x1 = qT[base + MLA_NOPE:base + MLA_NOPE + half, :]
        x2 = qT[base + MLA_NOPE + half:base + MLA_QK, :]
        qT_ref[0, h, 0, 0:MLA_NOPE, :] = (qT[base:base + MLA_NOPE, :] * scale).astype(BF16)
        qT_ref[0, h, 0, MLA_NOPE:MLA_NOPE + half, :] = ((x1 * cosT - x2 * sinT) * scale).astype(BF16)
        qT_ref[0, h, 0, MLA_NOPE + half:MLA_QK, :] = ((x2 * cosT + x1 * sinT) * scale).astype(BF16)

    ckvn = _rms(ckv_ref[...].astype(F32), gkv_ref[...]).astype(BF16)
    k_nope = jnp.dot(ckvn, wuk_ref[...], preferred_element_type=F32)
    vT = lax.dot_general(wuvT_ref[...], ckvn, nt, preferred_element_type=F32)

    krT = kr_ref[...].astype(F32).T
    k1 = krT[0:half, :]
    k2 = krT[half:MLA_ROPE, :]
    krT_rot = jnp.concatenate(
        [k1 * cosT - k2 * sinT, k2 * cosT + k1 * sinT, krT[MLA_ROPE:, :]], axis=0)
    kr = krT_rot.T[:, 0:MLA_ROPE].astype(BF16)

    for h in range(MLA_HEADS):
        k_ref[0, h, :, 0:MLA_NOPE] = k_nope[:, h * MLA_NOPE:(h + 1) * MLA_NOPE].astype(BF16)
        k_ref[0, h, :, MLA_NOPE:MLA_QK] = kr
        vT_ref[0, h, 0] = vT[h * MLA_V:(h + 1) * MLA_V, :].astype(BF16)


def _mla_prep(z, pos3, invf, g_q, g_kv, w_uqT, w_uk, w_uvT, batch, seq, tm):
    nblk = seq // tm
    hq = MLA_HEADS
    return pl.pallas_call(
        _mla_prep_kernel,
        out_shape=(jax.ShapeDtypeStruct((batch, hq, nblk, MLA_QK, tm), BF16),
                   jax.ShapeDtypeStruct((batch, hq, seq, MLA_QK), BF16),
                   jax.ShapeDtypeStruct((batch, hq, nblk, MLA_V, tm), BF16)),
        grid=(batch, nblk),
        in_specs=[pl.BlockSpec((1, 1, tm), lambda b, i: (b * nblk + i, 0, 0)),
                  pl.BlockSpec((MLA_ROPE // 2, 1), lambda b, i: (0, 0)),
                  pl.BlockSpec((tm, MLA_Q_RANK), lambda b, i: (b * nblk + i, Z_CQ // MLA_Q_RANK)),
                  pl.BlockSpec((tm, MLA_KV_RANK), lambda b, i: (b * nblk + i, Z_CKV // MLA_KV_RANK)),
                  pl.BlockSpec((tm, LANE), lambda b, i: (b * nblk + i, Z_KR // LANE)),
                  pl.BlockSpec((1, MLA_Q_RANK), lambda b, i: (0, 0)),
                  pl.BlockSpec((1, MLA_KV_RANK), lambda b, i: (0, 0)),
                  pl.BlockSpec(w_uqT.shape, lambda b, i: (0, 0)),
                  pl.BlockSpec(w_uk.shape, lambda b, i: (0, 0)),
                  pl.BlockSpec(w_uvT.shape, lambda b, i: (0, 0))],
        out_specs=(pl.BlockSpec((1, hq, 1, MLA_QK, tm), lambda b, i: (b, 0, i, 0, 0)),
                   pl.BlockSpec((1, hq, tm, MLA_QK), lambda b, i: (b, 0, i, 0)),
                   pl.BlockSpec((1, hq, 1, MLA_V, tm), lambda b, i: (b, 0, i, 0, 0))),
        compiler_params=_cparams(("arbitrary", "arbitrary")),
        name="mla_prep",
    )(pos3, invf, z, z, z, g_q, g_kv, w_uqT, w_uk, w_uvT)


def _attn_kernel(qT_ref, k_ref, vT_ref, o_ref, m_sc, l_sc, acc_sc):
    qi = pl.program_id(2)
    tq = qT_ref.shape[-1]
    tk = vT_ref.shape[-1]
    qT = qT_ref[0, 0, 0]

    m_sc[...] = jnp.full(m_sc.shape, NEG_INF, F32)
    l_sc[...] = jnp.zeros(l_sc.shape, F32)
    acc_sc[...] = jnp.zeros(acc_sc.shape, F32)

    def step(ki, masked):
        k = k_ref[0, 0, pl.ds(pl.multiple_of(ki * tk, tk), tk), :]
        sT = jnp.dot(k, qT, preferred_element_type=F32)
        if masked:
            kpos = lax.broadcasted_iota(jnp.int32, (tk, tq), 0)
            qpos = lax.broadcasted_iota(jnp.int32, (tk, tq), 1)
            sT = jnp.where(kpos <= qpos, sT, NEG_INF)
        m_prev = m_sc[...]
        m_new = jnp.maximum(m_prev, jnp.max(sT, axis=0, keepdims=True))
        alpha = jnp.exp(m_prev - m_new)
        p = jnp.exp(sT - m_new)
        l_sc[...] = alpha * l_sc[...] + jnp.sum(p, axis=0, keepdims=True)
        pv = jnp.dot(vT_ref[0, 0, ki], p.astype(BF16), preferred_element_type=F32)
        acc_sc[...] = alpha * acc_sc[...] + pv
        m_sc[...] = m_new

    def loop_body(ki, carry):
        step(ki, False)
        return carry

    lax.fori_loop(0, qi, loop_body, 0)
    step(qi, True)
    o_ref[0] = (acc_sc[...] / l_sc[...]).T.astype(BF16)


def _attention(qT, k, vT, batch, seq, tq):
    hq = MLA_HEADS
    nblk = seq // tq
    return pl.pallas_call(
        _attn_kernel,
        out_shape=jax.ShapeDtypeStruct((batch, seq, hq * MLA_V), BF16),
        grid=(batch, hq, nblk),
        in_specs=[pl.BlockSpec((1, 1, 1, MLA_QK, tq), lambda b, h, i: (b, h, i, 0, 0)),
                  pl.BlockSpec((1, 1, seq, MLA_QK), lambda b, h, i: (b, h, 0, 0)),
                  pl.BlockSpec((1, 1, nblk, MLA_V, tq), lambda b, h, i: (b, h, 0, 0, 0))],
        out_specs=pl.BlockSpec((1, tq, MLA_V), lambda b, h, i: (b, i, h)),
        scratch_shapes=[pltpu.VMEM((1, tq), F32), pltpu.VMEM((1, tq), F32),
                        pltpu.VMEM((MLA_V, tq), F32)],
        compiler_params=_cparams(("arbitrary", "arbitrary", "arbitrary")),
        name="mla_attention",
    )(qT, k, vT)


def _log_sigmoid(x):
    return jnp.minimum(x, 0.0) - jnp.log1p(jnp.exp(-jnp.abs(x)))


def _gla_kernel(q_ref, k_ref, v_ref, a_ref, r_ref, wg_ref, bg_ref, gg_ref, o_ref, st_sc):
    c_len = GLA_CHUNK
    nt = (((1,), (1,)), ((), ()))

    @pl.when(pl.program_id(1) == 0)
    def _():
        st_sc[...] = jnp.zeros(st_sc.shape, F32)

    row = lax.broadcasted_iota(jnp.int32, (c_len, c_len), 0)
    col = lax.broadcasted_iota(jnp.int32, (c_len, c_len), 1)
    causal = row >= col
    tril = causal.astype(BF16)
    wg = wg_ref[...]
    bg = bg_ref[...]
    gg = gg_ref[...]

    def chunk(c, carry):
        rows = pl.ds(pl.multiple_of(c * c_len, c_len), c_len)
        gate = jnp.dot(a_ref[rows, :], wg, preferred_element_type=F32) + bg
        log_a = _log_sigmoid(gate) / GLA_GATE_TAU
        hi = log_a.astype(BF16)
        lo = (log_a - hi.astype(F32)).astype(BF16)
        bc = (jnp.dot(tril, hi, preferred_element_type=F32)
              + jnp.dot(tril, lo, preferred_element_type=F32))
        b_last = bc[c_len - 1:c_len, :]
        q = q_ref[rows, :].astype(F32) * (GLA_DK ** -0.5)
        k = k_ref[rows, :].astype(F32)
        q_dec = (q * jnp.exp(bc)).astype(BF16)
        k_inv = (k * jnp.exp(-bc)).astype(BF16)
        k_dec = (k * jnp.exp(b_last - bc)).astype(BF16)
        decay = jnp.exp(b_last)
        for h in range(GLA_HEADS):
            ks = slice(h * GLA_DK, (h + 1) * GLA_DK)
            vs = slice(h * GLA_DV, (h + 1) * GLA_DV)
            v = v_ref[rows, vs]
            attn = lax.dot_general(q_dec[:, ks], k_inv[:, ks], nt, preferred_element_type=F32)
            attn = jnp.where(causal, attn, 0.0).astype(BF16)
            st = st_sc[h]
            o = (jnp.dot(attn, v, preferred_element_type=F32)
                 + lax.dot_general(q_dec[:, ks], st.astype(BF16), nt, preferred_element_type=F32))
            vT = v.astype(F32).T.astype(BF16)
            st_sc[h] = decay[:, ks] * st + jnp.dot(vT, k_dec[:, ks], preferred_element_type=F32)
            r = r_ref[rows, vs].astype(F32)
            o_ref[rows, vs] = (_rms(o, gg) * (r * jax.nn.sigmoid(r))).astype(BF16)
        return carry

    lax.fori_loop(0, q_ref.shape[0] // c_len, chunk, 0)


def _gla(z, w_gate, b_gate, g_gla, batch, seq, tb):
    nblk = seq // tb
    return pl.pallas_call(
        _gla_kernel,
        out_shape=jax.ShapeDtypeStruct((batch * seq, GLA_VAL), BF16),
        grid=(batch, nblk),
        in_specs=[pl.BlockSpec((tb, GLA_KEY), lambda b, i: (b * nblk + i, Z_QG // GLA_KEY)),
                  pl.BlockSpec((tb, GLA_KEY), lambda b, i: (b * nblk + i, Z_KG // GLA_KEY)),
                  pl.BlockSpec((tb, GLA_VAL), lambda b, i: (b * nblk + i, Z_VG // GLA_VAL)),
                  pl.BlockSpec((tb, LANE), lambda b, i: (b * nblk + i, Z_AG // LANE)),
                  pl.BlockSpec((tb, GLA_VAL), lambda b, i: (b * nblk + i, Z_RG // GLA_VAL)),
                  pl.BlockSpec((LANE, GLA_KEY), lambda b, i: (0, 0)),
                  pl.BlockSpec((1, GLA_KEY), lambda b, i: (0, 0)),
                  pl.BlockSpec((1, GLA_DV), lambda b, i: (0, 0))],
        out_specs=pl.BlockSpec((tb, GLA_VAL), lambda b, i: (b * nblk + i, 0)),
        scratch_shapes=[pltpu.VMEM((GLA_HEADS, GLA_DV, GLA_DK), F32)],
        compiler_params=_cparams(("arbitrary", "arbitrary")),
        name="gla",
    )(z, z, z, z, z, w_gate, b_gate, g_gla)


def _outproj_kernel(om_ref, og_ref, x_ref, w_ref, g_ref, gate_ref, o_ref):
    half = om_ref.shape[1]
    o = (jnp.dot(om_ref[...], w_ref[0:half, :], preferred_element_type=F32)
         + jnp.dot(og_ref[...], w_ref[half:, :], preferred_element_type=F32))
    o_ref[...] = x_ref[...] + gate_ref[0] * _rms(o, g_ref[...])


def _outproj(o_mla, o_gla, x2, w_out, g, mod, seq, tm):
    t, d = x2.shape
    per_b = seq // tm
    half = o_mla.shape[1]
    return pl.pallas_call(
        _outproj_kernel,
        out_shape=jax.ShapeDtypeStruct((t, d), F32),
        grid=(t // tm,),
        in_specs=[pl.BlockSpec((tm, half), lambda i: (i, 0)),
                  pl.BlockSpec((tm, half), lambda i: (i, 0)),
                  pl.BlockSpec((tm, d), lambda i: (i, 0)),
                  pl.BlockSpec(w_out.shape, lambda i: (0, 0)),
                  pl.BlockSpec((1, d), lambda i: (0, 0)),
                  pl.BlockSpec((1, 1, d), lambda i: ((i // per_b) * N_MOD + 2, 0, 0))],
        out_specs=pl.BlockSpec((tm, d), lambda i: (i, 0)),
        compiler_params=_cparams(("arbitrary",)),
        name="outproj",
    )(o_mla, o_gla, x2, w_out, g, mod)


def _ffn_kernel(x_ref, g_ref, scale_ref, shift_ref, wg_ref, wu_ref, wd_ref, gpost_ref, gate_ref,
                o_ref, h_ref, acc_ref):
    j = pl.program_id(1)

    @pl.when(j == 0)
    def _():
        _modulated_norm_to(h_ref, x_ref, g_ref, scale_ref, shift_ref)

    h = h_ref[...]
    a = jnp.dot(h, wg_ref[...], preferred_element_type=F32)
    u = jnp.dot(h, wu_ref[...], preferred_element_type=F32)
    act = (a * jax.nn.sigmoid(a) * u).astype(BF16)
    part = jnp.dot(act, wd_ref[...], preferred_element_type=F32)

    @pl.when(j == 0)
    def _():
        acc_ref[...] = part

    @pl.when(j > 0)
    def _():
        acc_ref[...] += part

    @pl.when(j == pl.num_programs(1) - 1)
    def _():
        gpost = gpost_ref[...]
        gate = gate_ref[0]

        def body(r, carry):
            rows = pl.ds(pl.multiple_of(r * ROW_CHUNK, ROW_CHUNK), ROW_CHUNK)
            o_ref[rows, :] = x_ref[rows, :] + gate * _rms(acc_ref[rows, :], gpost)
            return carry

        lax.fori_loop(0, x_ref.shape[0] // ROW_CHUNK, body, 0)


def _ffn(x1, g_pre, g_post, mod, wg, wu, wd, seq, tm, tf):
    t, d = x1.shape
    dff = wg.shape[1]
    per_b = seq // tm
    return pl.pallas_call(
        _ffn_kernel,
        out_shape=jax.ShapeDtypeStruct((t, d), F32),
        grid=(t // tm, dff // tf),
        in_specs=[pl.BlockSpec((tm, d), lambda i, j: (i, 0)),
                  pl.BlockSpec((1, d), lambda i, j: (0, 0)),
                  pl.BlockSpec((1, 1, d), lambda i, j: ((i // per_b) * N_MOD + 4, 0, 0)),
                  pl.BlockSpec((1, 1, d), lambda i, j: ((i // per_b) * N_MOD + 3, 0, 0)),
                  pl.BlockSpec((d, tf), lambda i, j: (0, j)),
                  pl.BlockSpec((d, tf), lambda i, j: (0, j)),
                  pl.BlockSpec((tf, d), lambda i, j: (j, 0)),
                  pl.BlockSpec((1, d), lambda i, j: (0, 0)),
                  pl.BlockSpec((1, 1, d), lambda i, j: ((i // per_b) * N_MOD + 5, 0, 0))],
        out_specs=pl.BlockSpec((tm, d), lambda i, j: (i, 0)),
        scratch_shapes=[pltpu.VMEM((tm, d), BF16), pltpu.VMEM((tm, d), F32)],
        compiler_params=_cparams(("arbitrary", "arbitrary")),
        name="ffn",
    )(x1, g_pre, mod, mod, wg, wu, wd, g_post, mod)


def _regroup_w_in(w_in):
    d = w_in.shape[0]
    c0 = MLA_Q_RANK
    c1 = c0 + MLA_KV_RANK
    c2 = c1 + MLA_ROPE
    c3 = c2 + GLA_KEY
    c4 = c3 + GLA_KEY
    c5 = c4 + GLA_VAL
    c6 = c5 + GLA_GATE_RANK
    c7 = c6 + GLA_VAL
    zeros = lambda n: jnp.zeros((d, n), w_in.dtype)
    parts = [w_in[:, 0:c1],
             w_in[:, c1:c2], zeros(LANE - MLA_ROPE),
             w_in[:, c5:c6], zeros(LANE - GLA_GATE_RANK),
             w_in[:, c2:c5],
             w_in[:, c6:c7]]
    return jnp.concatenate(parts, axis=1).astype(BF16)


def _tile(n, pref):
    return pref if n % pref == 0 else n


def _layer(x, c, positions, w_ada, b_ada, g_pre_mix, g_post_mix, w_in, g_q, w_uq, g_kv, w_uk, w_uv,
           w_gate_up, b_gate, g_gla, w_out, g_pre_ffn, g_post_ffn, w_ffn_gate, w_ffn_up, w_ffn_down):
    batch, seq, d = x.shape
    t = batch * seq
    row = lambda v: v.reshape(1, -1).astype(F32)

    c_pad = jnp.zeros((8, d), F32).at[:batch].set(c)
    ada = _ada(c_pad, w_ada, b_ada.reshape(1, -1))[:batch]
    mod = ada.reshape(batch * N_MOD, 1, d)

    x2 = x.reshape(t, d)

    tm_in = _tile(seq, 1024)
    z = _premix(x2, row(g_pre_mix), mod, _regroup_w_in(w_in), seq, tm_in, 512)

    tq = _tile(seq, 512)
    invf = (1.0 / (ROPE_THETA ** (np.arange(0, MLA_ROPE, 2, dtype=np.float32) / MLA_ROPE)))
    invf = jnp.asarray(invf, F32).reshape(-1, 1)
    pos3 = positions.reshape(t // tq, 1, tq)
    qT, k, vT = _mla_prep(z, pos3, invf, row(g_q), row(g_kv), w_uq.T.astype(BF16), w_uk.astype(BF16),
                          w_uv.T.astype(BF16), batch, seq, tq)
    o_mla = _attention(qT, k, vT, batch, seq, tq).reshape(t, MLA_HEADS * MLA_V)

    w_gate = jnp.zeros((LANE, GLA_KEY), BF16).at[:GLA_GATE_RANK].set(w_gate_up.astype(BF16))
    o_gla = _gla(z, w_gate, row(b_gate), row(g_gla), batch, seq, _tile(seq, 512))

    x1 = _outproj(o_mla, o_gla, x2, w_out.astype(BF16), row(g_post_mix), mod, seq, _tile(seq, 512))

    out = _ffn(x1, row(g_pre_ffn), row(g_post_ffn), mod, w_ffn_gate.astype(BF16), w_ffn_up.astype(BF16),
               w_ffn_down.astype(BF16), seq, _tile(seq, 512), 512)
    return out.reshape(batch, seq, d)


def kernel(x, c, positions, w_ada, b_ada, g_pre_mix, g_post_mix, w_in, g_q, w_uq, g_kv, w_uk, w_uv,
           w_gate_up, b_gate, g_gla, w_out, g_pre_ffn, g_post_ffn, w_ffn_gate, w_ffn_up, w_ffn_down):
    for l in range(w_ada.shape[0]):
        x = _layer(x, c, positions, w_ada[l], b_ada[l], g_pre_mix[l], g_post_mix[l], w_in[l], g_q[l],
                   w_uq[l], g_kv[l], w_uk[l], w_uv[l], w_gate_up[l], b_gate[l], g_gla[l], w_out[l],
                   g_pre_ffn[l], g_post_ffn[l], w_ffn_gate[l], w_ffn_up[l], w_ffn_down[l])
    return x
```

```python
import functools

import numpy as np
import jax
import jax.numpy as jnp
from jax import lax
from jax.experimental import pallas as pl
from jax.experimental.pallas import tpu as pltpu

F32 = jnp.float32
BF16 = jnp.bfloat16

MLA_HEADS = 8
MLA_Q_RANK = 512
MLA_KV_RANK = 256
MLA_NOPE = 128
MLA_ROPE = 64
MLA_QK = MLA_NOPE + MLA_ROPE
MLA_V = 128
V_ROWS = MLA_V + 16
LOG2_E = 1.4426950408889634
ATTN_HEADS_PER_STEP = 2
ROPE_THETA = 10000.0
NEG_INF = -1e30

GLA_HEADS = 4
GLA_DK = 128
GLA_DV = 256
GLA_KEY = GLA_HEADS * GLA_DK
GLA_VAL = GLA_HEADS * GLA_DV
GLA_GATE_RANK = 16
GLA_GATE_TAU = 16.0
GLA_CHUNK = 64

N_MOD = 6
RMS_EPS = 1e-6

LANE = 128
Z_CQ = 0
Z_CKV = 512
Z_KR = 768
Z_AG = 896
Z_QG = 1024
Z_KG = 1536
Z_VG = 2048
Z_RG = 3072
Z_COLS = 4096

VMEM_LIMIT = 56 * 1024 * 1024


def _cparams(sem):
    return pltpu.CompilerParams(dimension_semantics=sem, vmem_limit_bytes=VMEM_LIMIT)


def _rms(x, g):
    return x * lax.rsqrt(jnp.mean(x * x, axis=-1, keepdims=True) + RMS_EPS) * g


def _ada_kernel(c_ref, w_ref, b_ref, o_ref):
    c = c_ref[...]
    s = c * jax.nn.sigmoid(c)
    o_ref[...] = jnp.dot(s, w_ref[...], preferred_element_type=F32) + b_ref[...]


def _ada(c_pad, w_ada, b_ada):
    m, d = c_pad.shape
    n = w_ada.shape[1]
    tn = min(n, 1024)
    return pl.pallas_call(
        _ada_kernel,
        out_shape=jax.ShapeDtypeStruct((m, n), F32),
        grid=(n // tn,),
        in_specs=[pl.BlockSpec((m, d), lambda j: (0, 0)),
                  pl.BlockSpec((d, tn), lambda j: (0, j)),
                  pl.BlockSpec((1, tn), lambda j: (0, j))],
        out_specs=pl.BlockSpec((m, tn), lambda j: (0, j)),
        compiler_params=_cparams(("arbitrary",)),
        name="ada",
    )(c_pad, w_ada, b_ada)


ROW_CHUNK = 128


def _modulated_norm_to(h_ref, x_ref, g_ref, scale_ref, shift_ref):
    g = g_ref[...]
    one_plus = 1.0 + scale_ref[0]
    shift = shift_ref[0]

    def body(r, carry):
        rows = pl.ds(pl.multiple_of(r * ROW_CHUNK, ROW_CHUNK), ROW_CHUNK)
        h_ref[rows, :] = (_rms(x_ref[rows, :], g) * one_plus + shift).astype(BF16)
        return carry

    lax.fori_loop(0, x_ref.shape[0] // ROW_CHUNK, body, 0)


def _premix_kernel(x_ref, g_ref, scale_ref, shift_ref, w_ref, z_ref, h_ref):
    @pl.when(pl.program_id(1) == 0)
    def _():
        _modulated_norm_to(h_ref, x_ref, g_ref, scale_ref, shift_ref)

    z_ref[...] = jnp.dot(h_ref[...], w_ref[...], preferred_element_type=F32).astype(BF16)


def _premix(x2, g, mod, w_in_r, seq, tm, tn):
    t, d = x2.shape
    n = w_in_r.shape[1]
    per_b = seq // tm
    return pl.pallas_call(
        _premix_kernel,
        out_shape=jax.ShapeDtypeStruct((t, n), BF16),
        grid=(t // tm, n // tn),
        in_specs=[pl.BlockSpec((tm, d), lambda i, j: (i, 0)),
                  pl.BlockSpec((1, d), lambda i, j: (0, 0)),
                  pl.BlockSpec((1, 1, d), lambda i, j: ((i // per_b) * N_MOD + 1, 0, 0)),
                  pl.BlockSpec((1, 1, d), lambda i, j: ((i // per_b) * N_MOD + 0, 0, 0)),
                  pl.BlockSpec((d, tn), lambda i, j: (0, j))],
        out_specs=pl.BlockSpec((tm, tn), lambda i, j: (i, j)),
        scratch_shapes=[pltpu.VMEM((tm, d), BF16)],
        compiler_params=_cparams(("arbitrary", "arbitrary")),
        name="premix",
    )(x2, g, mod, mod, w_in_r)


def _mla_prep_kernel(pos_ref, invf_ref, cq_ref, ckv_ref, kr_ref, gq_ref, gkv_ref,
                     wuqT_ref, wuk_ref, wuvT_ref, qT_ref, k_ref, vT_ref):
    half = MLA_ROPE // 2
    ang = invf_ref[...] * pos_ref[0].astype(F32)
    cosT = jnp.cos(ang)
    sinT = jnp.sin(ang)

    nt = (((1,), (1,)), ((), ()))
    cqn = _rms(cq_ref[...].astype(F32), gq_ref[...]).astype(BF16)
    qT = lax.dot_general(wuqT_ref[...], cqn, nt, preferred_element_type=F32)
    scale = MLA_QK ** -0.5 * LOG2_E
    for h in range(MLA_HEADS):
        base = h * MLA_QK
        x1 = qT[base + MLA_NOPE:base + MLA_NOPE + half, :]
        x2 = qT[base + MLA_NOPE + half:base + MLA_QK, :]
        qT_ref[0, h, 0, 0:MLA_NOPE, :] = (qT[base:base + MLA_NOPE, :] * scale).astype(BF16)
        qT_ref[0, h, 0, MLA_NOPE:MLA_NOPE + half, :] = ((x1 * cosT - x2 * sinT) * scale).astype(BF16)
        qT_ref[0, h, 0, MLA_NOPE + half:MLA_QK, :] = ((x2 * cosT + x1 * sinT) * scale).astype(BF16)

    ckvn = _rms(ckv_ref[...].astype(F32), gkv_ref[...]).astype(BF16)
    k_nope = jnp.dot(ckvn, wuk_ref[...], preferred_element_type=F32)
    vT = lax.dot_general(wuvT_ref[...], ckvn, nt, preferred_element_type=F32)

    krT = kr_ref[...].astype(F32).T
    k1 = krT[0:half, :]
    k2 = krT[half:MLA_ROPE, :]
    krT_rot = jnp.concatenate(
        [k1 * cosT - k2 * sinT, k2 * cosT + k1 * sinT, krT[MLA_ROPE:, :]], axis=0)
    kr = krT_rot.T[:, 0:MLA_ROPE].astype(BF16)

    tm = vT.shape[1]
    ones_rows = (lax.broadcasted_iota(jnp.int32, (V_ROWS - MLA_V, tm), 0) == 0).astype(BF16)
    for h in range(MLA_HEADS):
        k_ref[0, h, :, 0:MLA_NOPE] = k_nope[:, h * MLA_NOPE:(h + 1) * MLA_NOPE].astype(BF16)
        k_ref[0, h, :, MLA_NOPE:MLA_QK] = kr
        vT_ref[0, h, 0, 0:MLA_V, :] = vT[h * MLA_V:(h + 1) * MLA_V, :].astype(BF16)
        vT_ref[0, h, 0, MLA_V:V_ROWS, :] = ones_rows


def _mla_prep(z, pos3, invf, g_q, g_kv, w_uqT, w_uk, w_uvT, batch, seq, tm):
    nblk = seq // tm
    hq = MLA_HEADS
    return pl.pallas_call(
        _mla_prep_kernel,
        out_shape=(jax.ShapeDtypeStruct((batch, hq, nblk, MLA_QK, tm), BF16),
                   jax.ShapeDtypeStruct((batch, hq, seq, MLA_QK), BF16),
                   jax.ShapeDtypeStruct((batch, hq, nblk, V_ROWS, tm), BF16)),
        grid=(batch, nblk),
        in_specs=[pl.BlockSpec((1, 1, tm), lambda b, i: (b * nblk + i, 0, 0)),
                  pl.BlockSpec((MLA_ROPE // 2, 1), lambda b, i: (0, 0)),
                  pl.BlockSpec((tm, MLA_Q_RANK), lambda b, i: (b * nblk + i, Z_CQ // MLA_Q_RANK)),
                  pl.BlockSpec((tm, MLA_KV_RANK), lambda b, i: (b * nblk + i, Z_CKV // MLA_KV_RANK)),
                  pl.BlockSpec((tm, LANE), lambda b, i: (b * nblk + i, Z_KR // LANE)),
                  pl.BlockSpec((1, MLA_Q_RANK), lambda b, i: (0, 0)),
                  pl.BlockSpec((1, MLA_KV_RANK), lambda b, i: (0, 0)),
                  pl.BlockSpec(w_uqT.shape, lambda b, i: (0, 0)),
                  pl.BlockSpec(w_uk.shape, lambda b, i: (0, 0)),
                  pl.BlockSpec(w_uvT.shape, lambda b, i: (0, 0))],
        out_specs=(pl.BlockSpec((1, hq, 1, MLA_QK, tm), lambda b, i: (b, 0, i, 0, 0)),
                   pl.BlockSpec((1, hq, tm, MLA_QK), lambda b, i: (b, 0, i, 0)),
                   pl.BlockSpec((1, hq, 1, V_ROWS, tm), lambda b, i: (b, 0, i, 0, 0))),
        compiler_params=_cparams(("arbitrary", "arbitrary")),
        name="mla_prep",
    )(pos3, invf, z, z, z, g_q, g_kv, w_uqT, w_uk, w_uvT)


def _attn_kernel(qT_ref, k_ref, vT_ref, o_ref, sa_sc, sb_sc, m_sc, acc_sc):
    qi = pl.program_id(2)
    tq = qT_ref.shape[-1]
    tk = vT_ref.shape[-1]
    heads = range(ATTN_HEADS_PER_STEP)

    m_sc[...] = jnp.full(m_sc.shape, NEG_INF, F32)
    acc_sc[...] = jnp.zeros(acc_sc.shape, F32)

    def scores(h, ki):
        k = k_ref[0, h, pl.ds(pl.multiple_of(ki * tk, tk), tk), :]
        return jnp.dot(k, qT_ref[0, h, 0], preferred_element_type=F32)

    def accumulate(s_ref, h, ki, masked):
        sT = s_ref[h]
        if masked:
            kpos = lax.broadcasted_iota(jnp.int32, (tk, tq), 0)
            qpos = lax.broadcasted_iota(jnp.int32, (tk, tq), 1)
            sT = jnp.where(kpos <= qpos, sT, NEG_INF)
        m_prev = m_sc[h]
        m_new = jnp.maximum(m_prev, jnp.max(sT, axis=0, keepdims=True))
        alpha = jnp.exp2(m_prev - m_new)
        p = jnp.exp2(sT - m_new).astype(BF16)
        acc_sc[h] = alpha * acc_sc[h] + jnp.dot(vT_ref[0, h, ki], p, preferred_element_type=F32)
        m_sc[h] = m_new

    for h in heads:
        sa_sc[h] = scores(h, 0)

    def pair(j, carry):
        for h in heads:
            sb_sc[h] = scores(h, 2 * j + 1)
        for h in heads:
            accumulate(sa_sc, h, 2 * j, False)
        for h in heads:
            sa_sc[h] = scores(h, 2 * j + 2)
        for h in heads:
            accumulate(sb_sc, h, 2 * j + 1, False)
        return carry

    lax.fori_loop(0, qi // 2, pair, 0)

    @pl.when(qi % 2 == 0)
    def _():
        for h in heads:
            accumulate(sa_sc, h, qi, True)

    @pl.when(qi % 2 == 1)
    def _():
        for h in heads:
            sb_sc[h] = scores(h, qi)
        for h in heads:
            accumulate(sa_sc, h, qi - 1, False)
        for h in heads:
            accumulate(sb_sc, h, qi, True)

    for h in heads:
        acc = acc_sc[h]
        o_ref[0, :, h * MLA_V:(h + 1) * MLA_V] = (
            acc[0:MLA_V, :] / acc[MLA_V:MLA_V + 1, :]).T.astype(BF16)


def _attention(qT, k, vT, batch, seq, tq):
    hq = MLA_HEADS
    hs = ATTN_HEADS_PER_STEP
    nblk = seq // tq
    return pl.pallas_call(
        _attn_kernel,
        out_shape=jax.ShapeDtypeStruct((batch, seq, hq * MLA_V), BF16),
        grid=(batch, hq // hs, nblk),
        in_specs=[pl.BlockSpec((1, hs, 1, MLA_QK, tq), lambda b, h, i: (b, h, i, 0, 0)),
                  pl.BlockSpec((1, hs, seq, MLA_QK), lambda b, h, i: (b, h, 0, 0)),
                  pl.BlockSpec((1, hs, nblk, V_ROWS, tq), lambda b, h, i: (b, h, 0, 0, 0))],
        out_specs=pl.BlockSpec((1, tq, hs * MLA_V), lambda b, h, i: (b, i, h)),
        scratch_shapes=[pltpu.VMEM((hs, tq, tq), F32), pltpu.VMEM((hs, tq, tq), F32),
                        pltpu.VMEM((hs, 1, tq), F32), pltpu.VMEM((hs, V_ROWS, tq), F32)],
        compiler_params=_cparams(("arbitrary", "arbitrary", "arbitrary")),
        name="mla_attention",
    )(qT, k, vT)


def _log_sigmoid(x):
    return jnp.minimum(x, 0.0) - jnp.log1p(jnp.exp(-jnp.abs(x)))


def _gla_kernel(q_ref, k_ref, v_ref, a_ref, r_ref, wg_ref, bg_ref, gg_ref, o_ref, st_sc):
    c_len = GLA_CHUNK
    nt = (((1,), (1,)), ((), ()))

    @pl.when(pl.program_id(1) == 0)
    def _():
        st_sc[...] = jnp.zeros(st_sc.shape, F32)

    row = lax.broadcasted_iota(jnp.int32, (c_len, c_len), 0)
    col = lax.broadcasted_iota(jnp.int32, (c_len, c_len), 1)
    causal = row >= col
    tril = causal.astype(BF16)
    wg = wg_ref[...]
    bg = bg_ref[...]
    gg = gg_ref[...]

    def chunk(c, carry):
        rows = pl.ds(pl.multiple_of(c * c_len, c_len), c_len)
        gate = jnp.dot(a_ref[rows, :], wg, preferred_element_type=F32) + bg
        log_a = _log_sigmoid(gate) / GLA_GATE_TAU
        hi = log_a.astype(BF16)
        lo = (log_a - hi.astype(F32)).astype(BF16)
        bc = (jnp.dot(tril, hi, preferred_element_type=F32)
              + jnp.dot(tril, lo, preferred_element_type=F32))
        b_last = bc[c_len - 1:c_len, :]
        q = q_ref[rows, :].astype(F32) * (GLA_DK ** -0.5)
        k = k_ref[rows, :].astype(F32)
        q_dec = (q * jnp.exp(bc)).astype(BF16)
        k_inv = (k * jnp.exp(-bc)).astype(BF16)
        k_dec = (k * jnp.exp(b_last - bc)).astype(BF16)
        decay = jnp.exp(b_last)
        for h in range(GLA_HEADS):
            ks = slice(h * GLA_DK, (h + 1) * GLA_DK)
            vs = slice(h * GLA_DV, (h + 1) * GLA_DV)
            v = v_ref[rows, vs]
            attn = lax.dot_general(q_dec[:, ks], k_inv[:, ks], nt, preferred_element_type=F32)
            attn = jnp.where(causal, attn, 0.0).astype(BF16)
            st = st_sc[h]
            o = (jnp.dot(attn, v, preferred_element_type=F32)
                 + lax.dot_general(q_dec[:, ks], st.astype(BF16), nt, preferred_element_type=F32))
            vT = v.astype(F32).T.astype(BF16)
            st_sc[h] = decay[:, ks] * st + jnp.dot(vT, k_dec[:, ks], preferred_element_type=F32)
            r = r_ref[rows, vs].astype(F32)
            o_ref[rows, vs] = (_rms(o, gg) * (r * jax.nn.sigmoid(r))).astype(BF16)
        return carry

    lax.fori_loop(0, q_ref.shape[0] // c_len, chunk, 0)


def _gla(z, w_gate, b_gate, g_gla, batch, seq, tb):
    nblk = seq // tb
    return pl.pallas_call(
        _gla_kernel,
        out_shape=jax.ShapeDtypeStruct((batch * seq, GLA_VAL), BF16),
        grid=(batch, nblk),
        in_specs=[pl.BlockSpec((tb, GLA_KEY), lambda b, i: (b * nblk + i, Z_QG // GLA_KEY)),
                  pl.BlockSpec((tb, GLA_KEY), lambda b, i: (b * nblk + i, Z_KG // GLA_KEY)),
                  pl.BlockSpec((tb, GLA_VAL), lambda b, i: (b * nblk + i, Z_VG // GLA_VAL)),
                  pl.BlockSpec((tb, LANE), lambda b, i: (b * nblk + i, Z_AG // LANE)),
                  pl.BlockSpec((tb, GLA_VAL), lambda b, i: (b * nblk + i, Z_RG // GLA_VAL)),
                  pl.BlockSpec((LANE, GLA_KEY), lambda b, i: (0, 0)),
                  pl.BlockSpec((1, GLA_KEY), lambda b, i: (0, 0)),
                  pl.BlockSpec((1, GLA_DV), lambda b, i: (0, 0))],
        out_specs=pl.BlockSpec((tb, GLA_VAL), lambda b, i: (b * nblk + i, 0)),
        scratch_shapes=[pltpu.VMEM((GLA_HEADS, GLA_DV, GLA_DK), F32)],
        compiler_params=_cparams(("arbitrary", "arbitrary")),
        name="gla",
    )(z, z, z, z, z, w_gate, b_gate, g_gla)


def _outproj_kernel(om_ref, og_ref, x_ref, w_ref, g_ref, gate_ref, o_ref):
    half = om_ref.shape[1]
    o = (jnp.dot(om_ref[...], w_ref[0:half, :], preferred_element_type=F32)
         + jnp.dot(og_ref[...], w_ref[half:, :], preferred_element_type=F32))
    o_ref[...] = x_ref[...] + gate_ref[0] * _rms(o, g_ref[...])


def _outproj(o_mla, o_gla, x2, w_out, g, mod, seq, tm):
    t, d = x2.shape
    per_b = seq // tm
    half = o_mla.shape[1]
    return pl.pallas_call(
        _outproj_kernel,
        out_shape=jax.ShapeDtypeStruct((t, d), F32),
        grid=(t // tm,),
        in_specs=[pl.BlockSpec((tm, half), lambda i: (i, 0)),
                  pl.BlockSpec((tm, half), lambda i: (i, 0)),
                  pl.BlockSpec((tm, d), lambda i: (i, 0)),
                  pl.BlockSpec(w_out.shape, lambda i: (0, 0)),
                  pl.BlockSpec((1, d), lambda i: (0, 0)),
                  pl.BlockSpec((1, 1, d), lambda i: ((i // per_b) * N_MOD + 2, 0, 0))],
        out_specs=pl.BlockSpec((tm, d), lambda i: (i, 0)),
        compiler_params=_cparams(("arbitrary",)),
        name="outproj",
    )(o_mla, o_gla, x2, w_out, g, mod)


def _ffn_kernel(x_ref, g_ref, scale_ref, shift_ref, wg_ref, wu_ref, wd_ref, gpost_ref, gate_ref,
                o_ref, h_ref, acc_ref):
    j = pl.program_id(1)

    @pl.when(j == 0)
    def _():
        _modulated_norm_to(h_ref, x_ref, g_ref, scale_ref, shift_ref)
        acc_ref[...] = jnp.zeros(acc_ref.shape, F32)

    h = h_ref[...]
    a = jnp.dot(h, wg_ref[...], preferred_element_type=F32)
    u = jnp.dot(h, wu_ref[...], preferred_element_type=F32)
    act = (a * jax.nn.sigmoid(a) * u).astype(BF16)
    acc_ref[...] += jnp.dot(act, wd_ref[...], preferred_element_type=F32)

    @pl.when(j == pl.num_programs(1) - 1)
    def _():
        gpost = gpost_ref[...]
        gate = gate_ref[0]

        def body(r, carry):
            rows = pl.ds(pl.multiple_of(r * ROW_CHUNK, ROW_CHUNK), ROW_CHUNK)
            o_ref[rows, :] = x_ref[rows, :] + gate * _rms(acc_ref[rows, :], gpost)
            return carry

        lax.fori_loop(0, x_ref.shape[0] // ROW_CHUNK, body, 0)


def _ffn(x1, g_pre, g_post, mod, wg, wu, wd, seq, tm, tf):
    t, d = x1.shape
    dff = wg.shape[1]
    per_b = seq // tm
    return pl.pallas_call(
        _ffn_kernel,
        out_shape=jax.ShapeDtypeStruct((t, d), F32),
        grid=(t // tm, dff // tf),
        in_specs=[pl.BlockSpec((tm, d), lambda i, j: (i, 0)),
                  pl.BlockSpec((1, d), lambda i, j: (0, 0)),
                  pl.BlockSpec((1, 1, d), lambda i, j: ((i // per_b) * N_MOD + 4, 0, 0)),
                  pl.BlockSpec((1, 1, d), lambda i, j: ((i // per_b) * N_MOD + 3, 0, 0)),
                  pl.BlockSpec((d, tf), lambda i, j: (0, j)),
                  pl.BlockSpec((d, tf), lambda i, j: (0, j)),
                  pl.BlockSpec((tf, d), lambda i, j: (j, 0)),
                  pl.BlockSpec((1, d), lambda i, j: (0, 0)),
                  pl.BlockSpec((1, 1, d), lambda i, j: ((i // per_b) * N_MOD + 5, 0, 0))],
        out_specs=pl.BlockSpec((tm, d), lambda i, j: (i, 0)),
        scratch_shapes=[pltpu.VMEM((tm, d), BF16), pltpu.VMEM((tm, d), F32)],
        compiler_params=_cparams(("arbitrary", "arbitrary")),
        name="ffn",
    )(x1, g_pre, mod, mod, wg, wu, wd, g_post, mod)


def _regroup_w_in(w_in):
    d = w_in.shape[0]
    c0 = MLA_Q_RANK
    c1 = c0 + MLA_KV_RANK
    c2 = c1 + MLA_ROPE
    c3 = c2 + GLA_KEY
    c4 = c3 + GLA_KEY
    c5 = c4 + GLA_VAL
    c6 = c5 + GLA_GATE_RANK
    c7 = c6 + GLA_VAL
    zeros = lambda n: jnp.zeros((d, n), w_in.dtype)
    parts = [w_in[:, 0:c1],
             w_in[:, c1:c2], zeros(LANE - MLA_ROPE),
             w_in[:, c5:c6], zeros(LANE - GLA_GATE_RANK),
             w_in[:, c2:c5],
             w_in[:, c6:c7]]
    return jnp.concatenate(parts, axis=1).astype(BF16)


def _tile(n, pref):
    return pref if n % pref == 0 else n


def _layer(x, c, positions, w_ada, b_ada, g_pre_mix, g_post_mix, w_in, g_q, w_uq, g_kv, w_uk, w_uv,
           w_gate_up, b_gate, g_gla, w_out, g_pre_ffn, g_post_ffn, w_ffn_gate, w_ffn_up, w_ffn_down):
    batch, seq, d = x.shape
    t = batch * seq
    row = lambda v: v.reshape(1, -1).astype(F32)

    c_pad = jnp.zeros((8, d), F32).at[:batch].set(c)
    ada = _ada(c_pad, w_ada, b_ada.reshape(1, -1))[:batch]
    mod = ada.reshape(batch * N_MOD, 1, d)

    x2 = x.reshape(t, d)

    tm_in = _tile(seq, 1024)
    z = _premix(x2, row(g_pre_mix), mod, _regroup_w_in(w_in), seq, tm_in, 512)

    tq = _tile(seq, 512)
    invf = (1.0 / (ROPE_THETA ** (np.arange(0, MLA_ROPE, 2, dtype=np.float32) / MLA_ROPE)))
    invf = jnp.asarray(invf, F32).reshape(-1, 1)
    pos3 = positions.reshape(t // tq, 1, tq)
    qT, k, vT = _mla_prep(z, pos3, invf, row(g_q), row(g_kv), w_uq.T.astype(BF16), w_uk.astype(BF16),
                          w_uv.T.astype(BF16), batch, seq, tq)
    o_mla = _attention(qT, k, vT, batch, seq, tq).reshape(t, MLA_HEADS * MLA_V)

    w_gate = jnp.zeros((LANE, GLA_KEY), BF16).at[:GLA_GATE_RANK].set(w_gate_up.astype(BF16))
    o_gla = _gla(z, w_gate, row(b_gate), row(g_gla), batch, seq, _tile(seq, 512))

    x1 = _outproj(o_mla, o_gla, x2, w_out.astype(BF16), row(g_post_mix), mod, seq, _tile(seq, 512))

    out = _ffn(x1, row(g_pre_ffn), row(g_post_ffn), mod, w_ffn_gate.astype(BF16), w_ffn_up.astype(BF16),
               w_ffn_down.astype(BF16), seq, _tile(seq, 512), 512)
    return out.reshape(batch, seq, d)


def kernel(x, c, positions, w_ada, b_ada, g_pre_mix, g_post_mix, w_in, g_q, w_uq, g_kv, w_uk, w_uv,
           w_gate_up, b_gate, g_gla, w_out, g_pre_ffn, g_post_ffn, w_ffn_gate, w_ffn_up, w_ffn_down):
    for l in range(w_ada.shape[0]):
        x = _layer(x, c, positions, w_ada[l], b_ada[l], g_pre_mix[l], g_post_mix[l], w_in[l], g_q[l],
                   w_uq[l], g_kv[l], w_uk[l], w_uv[l], w_gate_up[l], b_gate[l], g_gla[l], w_out[l],
                   g_pre_ffn[l], g_post_ffn[l], w_ffn_gate[l], w_ffn_up[l], w_ffn_down[l])
    return x
```

```python
import functools

import numpy as np
import jax
import jax.numpy as jnp
from jax import lax
from jax.experimental import pallas as pl
from jax.experimental.pallas import tpu as pltpu

F32 = jnp.float32
BF16 = jnp.bfloat16

MLA_HEADS = 8
MLA_Q_RANK = 512
MLA_KV_RANK = 256
MLA_NOPE = 128
MLA_ROPE = 64
MLA_QK = MLA_NOPE + MLA_ROPE
MLA_V = 128
V_ROWS = MLA_V + 16
LOG2_E = 1.4426950408889634
ATTN_HEADS_PER_STEP = 2
ROPE_THETA = 10000.0
NEG_INF = -1e30

GLA_HEADS = 4
GLA_DK = 128
GLA_DV = 256
GLA_KEY = GLA_HEADS * GLA_DK
GLA_VAL = GLA_HEADS * GLA_DV
GLA_GATE_RANK = 16
GLA_GATE_TAU = 16.0
GLA_CHUNK = 64

N_MOD = 6
RMS_EPS = 1e-6

LANE = 128
Z_CQ = 0
Z_CKV = 512
Z_KR = 768
Z_AG = 896
Z_QG = 1024
Z_KG = 1536
Z_VG = 2048
Z_RG = 3072
Z_COLS = 4096

VMEM_LIMIT = 56 * 1024 * 1024


def _cparams(sem):
    return pltpu.CompilerParams(dimension_semantics=sem, vmem_limit_bytes=VMEM_LIMIT)


def _rms(x, g):
    return x * lax.rsqrt(jnp.mean(x * x, axis=-1, keepdims=True) + RMS_EPS) * g


def _ada_kernel(c_ref, w_ref, b_ref, o_ref):
    c = c_ref[...]
    s = c * jax.nn.sigmoid(c)
    o_ref[...] = jnp.dot(s, w_ref[...], preferred_element_type=F32) + b_ref[...]


def _ada(c_pad, w_ada, b_ada):
    m, d = c_pad.shape
    n = w_ada.shape[1]
    tn = min(n, 1024)
    return pl.pallas_call(
        _ada_kernel,
        out_shape=jax.ShapeDtypeStruct((m, n), F32),
        grid=(n // tn,),
        in_specs=[pl.BlockSpec((m, d), lambda j: (0, 0)),
                  pl.BlockSpec((d, tn), lambda j: (0, j)),
                  pl.BlockSpec((1, tn), lambda j: (0, j))],
        out_specs=pl.BlockSpec((m, tn), lambda j: (0, j)),
        compiler_params=_cparams(("arbitrary",)),
        name="ada",
    )(c_pad, w_ada, b_ada)


ROW_CHUNK = 128


def _modulated_norm_to(h_ref, x_ref, g_ref, scale_ref, shift_ref):
    g = g_ref[...]
    one_plus = 1.0 + scale_ref[0]
    shift = shift_ref[0]

    def body(r, carry):
        rows = pl.ds(pl.multiple_of(r * ROW_CHUNK, ROW_CHUNK), ROW_CHUNK)
        h_ref[rows, :] = (_rms(x_ref[rows, :], g) * one_plus + shift).astype(BF16)
        return carry

    lax.fori_loop(0, x_ref.shape[0] // ROW_CHUNK, body, 0)


def _premix_kernel(x_ref, g_ref, scale_ref, shift_ref, w_ref, z_ref, h_ref):
    @pl.when(pl.program_id(1) == 0)
    def _():
        _modulated_norm_to(h_ref, x_ref, g_ref, scale_ref, shift_ref)

    z_ref[...] = jnp.dot(h_ref[...], w_ref[...], preferred_element_type=F32).astype(BF16)


def _premix(x2, g, mod, w_in_r, seq, tm, tn):
    t, d = x2.shape
    n = w_in_r.shape[1]
    per_b = seq // tm
    return pl.pallas_call(
        _premix_kernel,
        out_shape=jax.ShapeDtypeStruct((t, n), BF16),
        grid=(t // tm, n // tn),
        in_specs=[pl.BlockSpec((tm, d), lambda i, j: (i, 0)),
                  pl.BlockSpec((1, d), lambda i, j: (0, 0)),
                  pl.BlockSpec((1, 1, d), lambda i, j: ((i // per_b) * N_MOD + 1, 0, 0)),
                  pl.BlockSpec((1, 1, d), lambda i, j: ((i // per_b) * N_MOD + 0, 0, 0)),
                  pl.BlockSpec((d, tn), lambda i, j: (0, j))],
        out_specs=pl.BlockSpec((tm, tn), lambda i, j: (i, j)),
        scratch_shapes=[pltpu.VMEM((tm, d), BF16)],
        compiler_params=_cparams(("arbitrary", "arbitrary")),
        name="premix",
    )(x2, g, mod, mod, w_in_r)


def _mla_prep_kernel(pos_ref, invf_ref, cq_ref, ckv_ref, kr_ref, gq_ref, gkv_ref,
                     wuqT_ref, wuk_ref, wuvT_ref, qT_ref, k_ref, vT_ref):
    half = MLA_ROPE // 2
    ang = invf_ref[...] * pos_ref[0].astype(F32)
    cosT = jnp.cos(ang)
    sinT = jnp.sin(ang)

    nt = (((1,), (1,)), ((), ()))
    cqn = _rms(cq_ref[...].astype(F32), gq_ref[...]).astype(BF16)
    qT = lax.dot_general(wuqT_ref[...], cqn, nt, preferred_element_type=F32)
    scale = MLA_QK ** -0.5 * LOG2_E
    for h in range(MLA_HEADS):
        base = h * MLA_QK
        x1 = qT[base + MLA_NOPE:base + MLA_NOPE + half, :]
        x2 = qT[base + MLA_NOPE + half:base + MLA_QK, :]
        qT_ref[0, h, 0:MLA_NOPE, :] = (qT[base:base + MLA_NOPE, :] * scale).astype(BF16)
        qT_ref[0, h, MLA_NOPE:MLA_NOPE + half, :] = ((x1 * cosT - x2 * sinT) * scale).astype(BF16)
        qT_ref[0, h, MLA_NOPE + half:MLA_QK, :] = ((x2 * cosT + x1 * sinT) * scale).astype(BF16)

    ckvn = _rms(ckv_ref[...].astype(F32), gkv_ref[...]).astype(BF16)
    k_nope = jnp.dot(ckvn, wuk_ref[...], preferred_element_type=F32)
    vT = lax.dot_general(wuvT_ref[...], ckvn, nt, preferred_element_type=F32)

    krT = kr_ref[...].astype(F32).T
    k1 = krT[0:half, :]
    k2 = krT[half:MLA_ROPE, :]
    krT_rot = jnp.concatenate(
        [k1 * cosT - k2 * sinT, k2 * cosT + k1 * sinT, krT[MLA_ROPE:, :]], axis=0)
    kr = krT_rot.T[:, 0:MLA_ROPE].astype(BF16)

    tm = vT.shape[1]
    ones_rows = (lax.broadcasted_iota(jnp.int32, (V_ROWS - MLA_V, tm), 0) == 0).astype(BF16)
    for h in range(MLA_HEADS):
        k_ref[0, h, :, 0:MLA_NOPE] = k_nope[:, h * MLA_NOPE:(h + 1) * MLA_NOPE].astype(BF16)
        k_ref[0, h, :, MLA_NOPE:MLA_QK] = kr
        vT_ref[0, h, 0, 0:MLA_V, :] = vT[h * MLA_V:(h + 1) * MLA_V, :].astype(BF16)
        vT_ref[0, h, 0, MLA_V:V_ROWS, :] = ones_rows


def _mla_prep(z, pos3, invf, g_q, g_kv, w_uqT, w_uk, w_uvT, batch, seq, tm):
    nblk = seq // tm
    hq = MLA_HEADS
    return pl.pallas_call(
        _mla_prep_kernel,
        out_shape=(jax.ShapeDtypeStruct((batch, hq, MLA_QK, seq), BF16),
                   jax.ShapeDtypeStruct((batch, hq, seq, MLA_QK), BF16),
                   jax.ShapeDtypeStruct((batch, hq, nblk, V_ROWS, tm), BF16)),
        grid=(batch, nblk),
        in_specs=[pl.BlockSpec((1, 1, tm), lambda b, i: (b * nblk + i, 0, 0)),
                  pl.BlockSpec((MLA_ROPE // 2, 1), lambda b, i: (0, 0)),
                  pl.BlockSpec((tm, MLA_Q_RANK), lambda b, i: (b * nblk + i, Z_CQ // MLA_Q_RANK)),
                  pl.BlockSpec((tm, MLA_KV_RANK), lambda b, i: (b * nblk + i, Z_CKV // MLA_KV_RANK)),
                  pl.BlockSpec((tm, LANE), lambda b, i: (b * nblk + i, Z_KR // LANE)),
                  pl.BlockSpec((1, MLA_Q_RANK), lambda b, i: (0, 0)),
                  pl.BlockSpec((1, MLA_KV_RANK), lambda b, i: (0, 0)),
                  pl.BlockSpec(w_uqT.shape, lambda b, i: (0, 0)),
                  pl.BlockSpec(w_uk.shape, lambda b, i: (0, 0)),
                  pl.BlockSpec(w_uvT.shape, lambda b, i: (0, 0))],
        out_specs=(pl.BlockSpec((1, hq, MLA_QK, tm), lambda b, i: (b, 0, 0, i)),
                   pl.BlockSpec((1, hq, tm, MLA_QK), lambda b, i: (b, 0, i, 0)),
                   pl.BlockSpec((1, hq, 1, V_ROWS, tm), lambda b, i: (b, 0, i, 0, 0))),
        compiler_params=_cparams(("arbitrary", "arbitrary")),
        name="mla_prep",
    )(pos3, invf, z, z, z, g_q, g_kv, w_uqT, w_uk, w_uvT)


def _attn_kernel(qT_ref, qTn_ref, k_ref, vT_ref, o_ref, sa_sc, sb_sc, m_sc, acc_sc):
    qi = pl.program_id(2)
    tq = qT_ref.shape[-1]
    tk = vT_ref.shape[-1]
    heads = range(ATTN_HEADS_PER_STEP)

    m_sc[...] = jnp.full(m_sc.shape, NEG_INF, F32)
    acc_sc[...] = jnp.zeros(acc_sc.shape, F32)

    def scores(q_ref, h, ki):
        k = k_ref[0, h, pl.ds(pl.multiple_of(ki * tk, tk), tk), :]
        return jnp.dot(k, q_ref[0, h], preferred_element_type=F32)

    def accumulate(s_ref, h, ki, diag):
        sT = s_ref[h]
        if diag is not None:
            kpos = lax.broadcasted_iota(jnp.int32, (tk, tq), 0) + diag * tk
            qpos = lax.broadcasted_iota(jnp.int32, (tk, tq), 1)
            sT = jnp.where(kpos <= qpos, sT, NEG_INF)
        m_prev = m_sc[h]
        m_new = jnp.maximum(m_prev, jnp.max(sT, axis=0, keepdims=True))
        alpha = jnp.exp2(m_prev - m_new)
        p = jnp.exp2(sT - m_new).astype(BF16)
        acc_sc[h] = alpha * acc_sc[h] + jnp.dot(vT_ref[0, h, ki], p, preferred_element_type=F32)
        m_sc[h] = m_new

    @pl.when(qi == 0)
    def _():
        for h in heads:
            sa_sc[h] = scores(qT_ref, h, 0)

    def pair(j, carry):
        for h in heads:
            sb_sc[h] = scores(qT_ref, h, 2 * j + 1)
        for h in heads:
            accumulate(sa_sc, h, 2 * j, None)
        for h in heads:
            sa_sc[h] = scores(qT_ref, h, 2 * j + 2)
        for h in heads:
            accumulate(sb_sc, h, 2 * j + 1, None)
        return carry

    lax.fori_loop(0, qi, pair, 0)

    for h in heads:
        sb_sc[h] = scores(qT_ref, h, 2 * qi + 1)
    for h in heads:
        accumulate(sa_sc, h, 2 * qi, 0)
    for h in heads:
        sa_sc[h] = scores(qTn_ref, h, 0)
    for h in heads:
        accumulate(sb_sc, h, 2 * qi + 1, 1)

    for h in heads:
        acc = acc_sc[h]
        o_ref[0, :, h * MLA_V:(h + 1) * MLA_V] = (
            acc[0:MLA_V, :] / acc[MLA_V:MLA_V + 1, :]).T.astype(BF16)


def _attention(qT, k, vT, batch, seq, tk):
    hq = MLA_HEADS
    hs = ATTN_HEADS_PER_STEP
    tq = 2 * tk
    nq = seq // tq
    return pl.pallas_call(
        _attn_kernel,
        out_shape=jax.ShapeDtypeStruct((batch, seq, hq * MLA_V), BF16),
        grid=(batch, hq // hs, nq),
        in_specs=[pl.BlockSpec((1, hs, MLA_QK, tq), lambda b, h, i: (b, h, 0, i)),
                  pl.BlockSpec((1, hs, MLA_QK, tq), lambda b, h, i: (b, h, 0, jnp.minimum(i + 1, nq - 1))),
                  pl.BlockSpec((1, hs, seq, MLA_QK), lambda b, h, i: (b, h, 0, 0)),
                  pl.BlockSpec((1, hs, seq // tk, V_ROWS, tk), lambda b, h, i: (b, h, 0, 0, 0))],
        out_specs=pl.BlockSpec((1, tq, hs * MLA_V), lambda b, h, i: (b, i, h)),
        scratch_shapes=[pltpu.VMEM((hs, tk, tq), F32), pltpu.VMEM((hs, tk, tq), F32),
                        pltpu.VMEM((hs, 1, tq), F32), pltpu.VMEM((hs, V_ROWS, tq), F32)],
        compiler_params=_cparams(("arbitrary", "arbitrary", "arbitrary")),
        name="mla_attention",
    )(qT, qT, k, vT)


def _log_sigmoid(x):
    return jnp.minimum(x, 0.0) - jnp.log1p(jnp.exp(-jnp.abs(x)))


def _gla_kernel(q_ref, k_ref, v_ref, a_ref, r_ref, wg_ref, bg_ref, gg_ref, o_ref, st_sc):
    c_len = GLA_CHUNK
    nt = (((1,), (1,)), ((), ()))

    @pl.when(pl.program_id(1) == 0)
    def _():
        st_sc[...] = jnp.zeros(st_sc.shape, F32)

    row = lax.broadcasted_iota(jnp.int32, (c_len, c_len), 0)
    col = lax.broadcasted_iota(jnp.int32, (c_len, c_len), 1)
    causal = row >= col
    tril = causal.astype(BF16)
    wg = wg_ref[...]
    bg = bg_ref[...]
    gg = gg_ref[...]

    def chunk(c):
        rows = pl.ds(c * c_len, c_len)
        gate = jnp.dot(a_ref[rows, :], wg, preferred_element_type=F32) + bg
        log_a = _log_sigmoid(gate) / GLA_GATE_TAU
        hi = log_a.astype(BF16)
        lo = (log_a - hi.astype(F32)).astype(BF16)
        bc = (jnp.dot(tril, hi, preferred_element_type=F32)
              + jnp.dot(tril, lo, preferred_element_type=F32))
        b_last = bc[c_len - 1:c_len, :]
        q = q_ref[rows, :].astype(F32) * (GLA_DK ** -0.5)
        k = k_ref[rows, :].astype(F32)
        q_dec = (q * jnp.exp(bc)).astype(BF16)
        k_inv = (k * jnp.exp(-bc)).astype(BF16)
        k_dec = (k * jnp.exp(b_last - bc)).astype(BF16)
        decay = jnp.exp(b_last)
        for h in range(GLA_HEADS):
            ks = slice(h * GLA_DK, (h + 1) * GLA_DK)
            vs = slice(h * GLA_DV, (h + 1) * GLA_DV)
            v = v_ref[rows, vs]
            attn = lax.dot_general(q_dec[:, ks], k_inv[:, ks], nt, preferred_element_type=F32)
            attn = jnp.where(causal, attn, 0.0).astype(BF16)
            st = st_sc[h]
            o = (jnp.dot(attn, v, preferred_element_type=F32)
                 + lax.dot_general(q_dec[:, ks], st.astype(BF16), nt, preferred_element_type=F32))
            vT = v.astype(F32).T.astype(BF16)
            st_sc[h] = decay[:, ks] * st + jnp.dot(vT, k_dec[:, ks], preferred_element_type=F32)
            r = r_ref[rows, vs].astype(F32)
            o_ref[rows, vs] = (_rms(o, gg) * (r * jax.nn.sigmoid(r))).astype(BF16)

    for c in range(q_ref.shape[0] // c_len):
        chunk(c)


def _gla(z, w_gate, b_gate, g_gla, batch, seq, tb):
    nblk = seq // tb
    return pl.pallas_call(
        _gla_kernel,
        out_shape=jax.ShapeDtypeStruct((batch * seq, GLA_VAL), BF16),
        grid=(batch, nblk),
        in_specs=[pl.BlockSpec((tb, GLA_KEY), lambda b, i: (b * nblk + i, Z_QG // GLA_KEY)),
                  pl.BlockSpec((tb, GLA_KEY), lambda b, i: (b * nblk + i, Z_KG // GLA_KEY)),
                  pl.BlockSpec((tb, GLA_VAL), lambda b, i: (b * nblk + i, Z_VG // GLA_VAL)),
                  pl.BlockSpec((tb, LANE), lambda b, i: (b * nblk + i, Z_AG // LANE)),
                  pl.BlockSpec((tb, GLA_VAL), lambda b, i: (b * nblk + i, Z_RG // GLA_VAL)),
                  pl.BlockSpec((LANE, GLA_KEY), lambda b, i: (0, 0)),
                  pl.BlockSpec((1, GLA_KEY), lambda b, i: (0, 0)),
                  pl.BlockSpec((1, GLA_DV), lambda b, i: (0, 0))],
        out_specs=pl.BlockSpec((tb, GLA_VAL), lambda b, i: (b * nblk + i, 0)),
        scratch_shapes=[pltpu.VMEM((GLA_HEADS, GLA_DV, GLA_DK), F32)],
        compiler_params=_cparams(("arbitrary", "arbitrary")),
        name="gla",
    )(z, z, z, z, z, w_gate, b_gate, g_gla)


def _outproj_kernel(om_ref, og_ref, x_ref, w_ref, g_ref, gate_ref, o_ref):
    half = om_ref.shape[1]
    o = (jnp.dot(om_ref[...], w_ref[0:half, :], preferred_element_type=F32)
         + jnp.dot(og_ref[...], w_ref[half:, :], preferred_element_type=F32))
    o_ref[...] = x_ref[...] + gate_ref[0] * _rms(o, g_ref[...])


def _outproj(o_mla, o_gla, x2, w_out, g, mod, seq, tm):
    t, d = x2.shape
    per_b = seq // tm
    half = o_mla.shape[1]
    return pl.pallas_call(
        _outproj_kernel,
        out_shape=jax.ShapeDtypeStruct((t, d), F32),
        grid=(t // tm,),
        in_specs=[pl.BlockSpec((tm, half), lambda i: (i, 0)),
                  pl.BlockSpec((tm, half), lambda i: (i, 0)),
                  pl.BlockSpec((tm, d), lambda i: (i, 0)),
                  pl.BlockSpec(w_out.shape, lambda i: (0, 0)),
                  pl.BlockSpec((1, d), lambda i: (0, 0)),
                  pl.BlockSpec((1, 1, d), lambda i: ((i // per_b) * N_MOD + 2, 0, 0))],
        out_specs=pl.BlockSpec((tm, d), lambda i: (i, 0)),
        compiler_params=_cparams(("arbitrary",)),
        name="outproj",
    )(o_mla, o_gla, x2, w_out, g, mod)


def _ffn_kernel(x_ref, g_ref, scale_ref, shift_ref, wg_ref, wu_ref, wd_ref, gpost_ref, gate_ref,
                o_ref, h_ref, acc_ref):
    j = pl.program_id(1)

    @pl.when(j == 0)
    def _():
        _modulated_norm_to(h_ref, x_ref, g_ref, scale_ref, shift_ref)
        acc_ref[...] = jnp.zeros(acc_ref.shape, F32)

    h = h_ref[...]
    a = jnp.dot(h, wg_ref[...], preferred_element_type=F32)
    u = jnp.dot(h, wu_ref[...], preferred_element_type=F32)
    act = (a * jax.nn.sigmoid(a) * u).astype(BF16)
    acc_ref[...] += jnp.dot(act, wd_ref[...], preferred_element_type=F32)

    @pl.when(j == pl.num_programs(1) - 1)
    def _():
        gpost = gpost_ref[...]
        gate = gate_ref[0]

        def body(r, carry):
            rows = pl.ds(pl.multiple_of(r * ROW_CHUNK, ROW_CHUNK), ROW_CHUNK)
            o_ref[rows, :] = x_ref[rows, :] + gate * _rms(acc_ref[rows, :], gpost)
            return carry

        lax.fori_loop(0, x_ref.shape[0] // ROW_CHUNK, body, 0)


def _ffn(x1, g_pre, g_post, mod, wg, wu, wd, seq, tm, tf):
    t, d = x1.shape
    dff = wg.shape[1]
    per_b = seq // tm
    return pl.pallas_call(
        _ffn_kernel,
        out_shape=jax.ShapeDtypeStruct((t, d), F32),
        grid=(t // tm, dff // tf),
        in_specs=[pl.BlockSpec((tm, d), lambda i, j: (i, 0)),
                  pl.BlockSpec((1, d), lambda i, j: (0, 0)),
                  pl.BlockSpec((1, 1, d), lambda i, j: ((i // per_b) * N_MOD + 4, 0, 0)),
                  pl.BlockSpec((1, 1, d), lambda i, j: ((i // per_b) * N_MOD + 3, 0, 0)),
                  pl.BlockSpec((d, tf), lambda i, j: (0, j)),
                  pl.BlockSpec((d, tf), lambda i, j: (0, j)),
                  pl.BlockSpec((tf, d), lambda i, j: (j, 0)),
                  pl.BlockSpec((1, d), lambda i, j: (0, 0)),
                  pl.BlockSpec((1, 1, d), lambda i, j: ((i // per_b) * N_MOD + 5, 0, 0))],
        out_specs=pl.BlockSpec((tm, d), lambda i, j: (i, 0)),
        scratch_shapes=[pltpu.VMEM((tm, d), BF16), pltpu.VMEM((tm, d), F32)],
        compiler_params=_cparams(("arbitrary", "arbitrary")),
        name="ffn",
    )(x1, g_pre, mod, mod, wg, wu, wd, g_post, mod)


def _regroup_w_in(w_in):
    d = w_in.shape[0]
    c0 = MLA_Q_RANK
    c1 = c0 + MLA_KV_RANK
    c2 = c1 + MLA_ROPE
    c3 = c2 + GLA_KEY
    c4 = c3 + GLA_KEY
    c5 = c4 + GLA_VAL
    c6 = c5 + GLA_GATE_RANK
    c7 = c6 + GLA_VAL
    zeros = lambda n: jnp.zeros((d, n), w_in.dtype)
    parts = [w_in[:, 0:c1],
             w_in[:, c1:c2], zeros(LANE - MLA_ROPE),
             w_in[:, c5:c6], zeros(LANE - GLA_GATE_RANK),
             w_in[:, c2:c5],
             w_in[:, c6:c7]]
    return jnp.concatenate(parts, axis=1).astype(BF16)


def _tile(n, pref):
    return pref if n % pref == 0 else n


def _layer(x, c, positions, w_ada, b_ada, g_pre_mix, g_post_mix, w_in, g_q, w_uq, g_kv, w_uk, w_uv,
           w_gate_up, b_gate, g_gla, w_out, g_pre_ffn, g_post_ffn, w_ffn_gate, w_ffn_up, w_ffn_down):
    batch, seq, d = x.shape
    t = batch * seq
    row = lambda v: v.reshape(1, -1).astype(F32)

    c_pad = jnp.zeros((8, d), F32).at[:batch].set(c)
    ada = _ada(c_pad, w_ada, b_ada.reshape(1, -1))[:batch]
    mod = ada.reshape(batch * N_MOD, 1, d)

    x2 = x.reshape(t, d)

    tm_in = _tile(seq, 1024)
    z = _premix(x2, row(g_pre_mix), mod, _regroup_w_in(w_in), seq, tm_in, 512)

    tq = _tile(seq, 512)
    invf = (1.0 / (ROPE_THETA ** (np.arange(0, MLA_ROPE, 2, dtype=np.float32) / MLA_ROPE)))
    invf = jnp.asarray(invf, F32).reshape(-1, 1)
    pos3 = positions.reshape(t // tq, 1, tq)
    qT, k, vT = _mla_prep(z, pos3, invf, row(g_q), row(g_kv), w_uq.T.astype(BF16), w_uk.astype(BF16),
                          w_uv.T.astype(BF16), batch, seq, tq)
    o_mla = _attention(qT, k, vT, batch, seq, tq).reshape(t, MLA_HEADS * MLA_V)

    w_gate = jnp.zeros((LANE, GLA_KEY), BF16).at[:GLA_GATE_RANK].set(w_gate_up.astype(BF16))
    o_gla = _gla(z, w_gate, row(b_gate), row(g_gla), batch, seq, _tile(seq, 512))

    x1 = _outproj(o_mla, o_gla, x2, w_out.astype(BF16), row(g_post_mix), mod, seq, _tile(seq, 512))

    out = _ffn(x1, row(g_pre_ffn), row(g_post_ffn), mod, w_ffn_gate.astype(BF16), w_ffn_up.astype(BF16),
               w_ffn_down.astype(BF16), seq, _tile(seq, 512), 512)
    return out.reshape(batch, seq, d)


def kernel(x, c, positions, w_ada, b_ada, g_pre_mix, g_post_mix, w_in, g_q, w_uq, g_kv, w_uk, w_uv,
           w_gate_up, b_gate, g_gla, w_out, g_pre_ffn, g_post_ffn, w_ffn_gate, w_ffn_up, w_ffn_down):
    for l in range(w_ada.shape[0]):
        x = _layer(x, c, positions, w_ada[l], b_ada[l], g_pre_mix[l], g_post_mix[l], w_in[l], g_q[l],
                   w_uq[l], g_kv[l], w_uk[l], w_uv[l], w_gate_up[l], b_gate[l], g_gla[l], w_out[l],
                   g_pre_ffn[l], g_post_ffn[l], w_ffn_gate[l], w_ffn_up[l], w_ffn_down[l])
    return x
```
